```python
import math
import jax, jax.numpy as jnp
from jax import lax
import numpy as np

D_MODEL = 2048
BATCH = 1
SEQ = 16384
DEPTH = 4

CHUNK = 64
QBLOCK = 128
D_MIX = D_MODEL
A_HEADS = 6
A_HEAD_DIM = 64
A_WIDTH = A_HEADS * 2 * A_HEAD_DIM
B_HEADS = 6
B_HEAD_DIM = 128
B_WIDTH = B_HEADS * B_HEAD_DIM
IDX_HEADS = 8
IDX_DIM = 64
DSA_TOPK_MAX = 256
CONV_CH = D_MIX - A_WIDTH - B_WIDTH
CONV_WIDTH = 31
N_IN = 3 * A_WIDTH + 3 * B_WIDTH + IDX_HEADS * IDX_DIM + IDX_DIM + IDX_HEADS + 2 * CONV_CH
N_GROUPS = 4
EXPERTS_PER_GROUP = 8
N_EXPERTS = N_GROUPS * EXPERTS_PER_GROUP
TOPK_IN_GROUP = 2
D_EXPERT = 512
MOE_BLOCK = 128
ROPE_THETA = 10000.0
NORM_EPS = 1e-6
LN_EPS = 1e-5

kernel_name = 'hybrid_diff_dsa_conformer_hmoe'


def rms_norm(x, g, eps=NORM_EPS):
    xf = x.astype(jnp.float32)
    y = xf * lax.rsqrt(jnp.mean(xf * xf, axis=-1, keepdims=True) + eps)
    return (y * g.astype(jnp.float32)).astype(x.dtype)


def layer_norm(x, g, b, eps=LN_EPS):
    xf = x.astype(jnp.float32)
    mu = jnp.mean(xf, axis=-1, keepdims=True)
    var = jnp.mean(jnp.square(xf - mu), axis=-1, keepdims=True)
    y = (xf - mu) * lax.rsqrt(var + eps)
    return (y * g.astype(jnp.float32) + b.astype(jnp.float32)).astype(x.dtype)


def rope_tables(positions, dim):
    inv_freq = ROPE_THETA ** (-jnp.arange(0, dim, 2, dtype=jnp.float32) / dim)
    ang = positions.astype(jnp.float32)[..., None] * inv_freq
    return jnp.cos(ang), jnp.sin(ang)


def apply_rope(x, cos, sin):
    xf = x.astype(jnp.float32)
    x1, x2 = jnp.split(xf, 2, axis=-1)
    c = cos[:, :, None, :]
    s = sin[:, :, None, :]
    return jnp.concatenate([x1 * c - x2 * s, x2 * c + x1 * s], axis=-1).astype(x.dtype)


def to_blocks(a):
    b, t = a.shape[:2]
    a = a.reshape((b, t // QBLOCK, QBLOCK) + a.shape[2:])
    return jnp.moveaxis(a, 1, 0)


def from_blocks(a):
    a = jnp.moveaxis(a, 0, 1)
    return a.reshape((a.shape[0], a.shape[1] * a.shape[2]) + a.shape[3:])


def chunk_visible(block_idx, n_keys):
    q_chunk = (block_idx * QBLOCK + jnp.arange(QBLOCK)) // CHUNK
    k_chunk = jnp.arange(n_keys) // CHUNK
    return k_chunk[None, :] <= q_chunk[:, None]


def masked_softmax(s, mask):
    return jax.nn.softmax(jnp.where(mask, s.astype(jnp.float32), -jnp.inf), axis=-1)


def differential_attention(q1, q2, k1, k2, v, lam):
    n_keys = k1.shape[1]
    scale = q1.shape[-1] ** -0.5

    def block(args):
        q1b, q2b, bi = args
        mask = chunk_visible(bi, n_keys)
        p1 = masked_softmax(jnp.einsum('bqhd,bkhd->bhqk', q1b, k1) * scale, mask)
        p2 = masked_softmax(jnp.einsum('bqhd,bkhd->bhqk', q2b, k2) * scale, mask)
        return jnp.einsum('bhqk,bkhe->bqhe', (p1 - lam * p2).astype(v.dtype), v)

    nb = q1.shape[1] // QBLOCK
    out = lax.map(block, (to_blocks(q1), to_blocks(q2), jnp.arange(nb)))
    return from_blocks(out)


def dsa_attention(q, k, v, q_idx, k_idx, w_idx, topk):
    n_keys = k.shape[1]
    k_chunk = jnp.arange(n_keys) // CHUNK
    scale = q.shape[-1] ** -0.5
    idx_scale = q_idx.shape[-1] ** -0.5
    gather = jax.vmap(lambda table, idx: table[idx])

    def block(args):
        qb, qib, wib, bi = args
        q_chunk = (bi * QBLOCK + jnp.arange(QBLOCK)) // CHUNK
        admissible = k_chunk[None, :] <= q_chunk[:, None]
        logits = jnp.einsum('bqhd,bkd->bqhk', qib, k_idx).astype(jnp.float32) * idx_scale
        score = jnp.einsum('bqh,bqhk->bqk', wib.astype(jnp.float32), jax.nn.relu(logits))
        score = jnp.where(admissible, score, -jnp.inf)
        _, sel = lax.top_k(score, topk)
        valid = (sel // CHUNK) <= q_chunk[None, :, None]
        kg = gather(k, sel)
        vg = gather(v, sel)
        s = jnp.einsum('bqhd,bqkhd->bhqk', qb, kg) * scale
        p = masked_softmax(s, valid[:, None])
        return jnp.einsum('bhqk,bqkhd->bqhd', p.astype(vg.dtype), vg)

    nb = q.shape[1] // QBLOCK
    out = lax.map(block, (to_blocks(q), to_blocks(q_idx), to_blocks(w_idx), jnp.arange(nb)))
    return from_blocks(out)


def conformer_conv(u, w_dw, b_dw, ln_g, ln_b):
    a, g = jnp.split(u, 2, axis=-1)
    y = a * jax.nn.sigmoid(g)
    y = lax.conv_general_dilated(
        y, w_dw[:, None, :].astype(y.dtype), window_strides=(1,),
        padding=[(CONV_WIDTH - 1, 0)], dimension_numbers=('NWC', 'WIO', 'NWC'),
        feature_group_count=CONV_CH) + b_dw.astype(y.dtype)
    return jax.nn.silu(layer_norm(y, ln_g, ln_b))


def hybrid_mixer(h, rope_a, rope_b, rope_i, w_in, w_out, lq1, lk1, lq2, lk2, subln_g,
                 conv_w, conv_b, conv_ln_g, conv_ln_b, lam_init, topk):
    bsz, seqlen, _ = h.shape
    sizes = [A_WIDTH] * 3 + [B_WIDTH] * 3 + [IDX_HEADS * IDX_DIM, IDX_DIM, IDX_HEADS, 2 * CONV_CH]
    splits = np.cumsum(sizes)[:-1].tolist()
    aq, ak, av, bq, bk, bv, iq, ik, iw, cu = jnp.split(h @ w_in, splits, axis=-1)

    cos_a, sin_a = rope_a
    aq = apply_rope(aq.reshape(bsz, seqlen, 2 * A_HEADS, A_HEAD_DIM), cos_a, sin_a)
    ak = apply_rope(ak.reshape(bsz, seqlen, 2 * A_HEADS, A_HEAD_DIM), cos_a, sin_a)
    aq = aq.reshape(bsz, seqlen, A_HEADS, 2, A_HEAD_DIM)
    ak = ak.reshape(bsz, seqlen, A_HEADS, 2, A_HEAD_DIM)
    f32 = jnp.float32
    lam = (jnp.exp(jnp.sum(lq1.astype(f32) * lk1.astype(f32)))
           - jnp.exp(jnp.sum(lq2.astype(f32) * lk2.astype(f32))) + lam_init)
    ao = differential_attention(aq[..., 0, :], aq[..., 1, :], ak[..., 0, :], ak[..., 1, :],
                                av.reshape(bsz, seqlen, A_HEADS, 2 * A_HEAD_DIM), lam)
    ao = (rms_norm(ao, subln_g, LN_EPS) * (1.0 - lam_init)).reshape(bsz, seqlen, A_WIDTH)

    cos_b, sin_b = rope_b
    cos_i, sin_i = rope_i
    bq = apply_rope(bq.reshape(bsz, seqlen, B_HEADS, B_HEAD_DIM), cos_b, sin_b)
    bk = apply_rope(bk.reshape(bsz, seqlen, B_HEADS, B_HEAD_DIM), cos_b, sin_b)
    bv = bv.reshape(bsz, seqlen, B_HEADS, B_HEAD_DIM)
    iq = apply_rope(iq.reshape(bsz, seqlen, IDX_HEADS, IDX_DIM), cos_i, sin_i)
    ik = apply_rope(ik[:, :, None, :], cos_i, sin_i)[:, :, 0, :]
    iw = iw * (IDX_HEADS ** -0.5)
    bo = dsa_attention(bq, bk, bv, iq, ik, iw, topk).reshape(bsz, seqlen, B_WIDTH)

    co = conformer_conv(cu, conv_w, conv_b, conv_ln_g, conv_ln_b)

    return jnp.concatenate([ao, bo, co], axis=-1) @ w_out


def grouped_expert_ffn(xt, expert, gates, w_gate, w_up, w_down):
    n, d = xt.shape
    k = expert.shape[1]
    a = n * k
    flat_e = expert.reshape(-1).astype(jnp.int32)
    order = jnp.argsort(flat_e).astype(jnp.int32)
    sorted_e = flat_e[order]
    counts = jnp.bincount(flat_e, length=N_EXPERTS).astype(jnp.int32)
    padded = (counts + MOE_BLOCK - 1) // MOE_BLOCK * MOE_BLOCK
    padded_end = jnp.cumsum(padded)
    padded_start = padded_end - padded
    start = jnp.cumsum(counts) - counts
    dest_sorted = padded_start[sorted_e] + jnp.arange(a, dtype=jnp.int32) - start[sorted_e]
    p = (a + MOE_BLOCK - 1) // MOE_BLOCK * MOE_BLOCK + N_EXPERTS * MOE_BLOCK
    nblk = p // MOE_BLOCK
    row_token = jnp.full((p,), n, jnp.int32).at[dest_sorted].set(order // k)
    x_pad = jnp.concatenate([xt, jnp.zeros((1, d), xt.dtype)], axis=0)
    xb = x_pad[row_token].reshape(nblk, MOE_BLOCK, d)
    blk_e = jnp.minimum(
        jnp.searchsorted(padded_end, jnp.arange(nblk, dtype=jnp.int32) * MOE_BLOCK, side='right'),
        N_EXPERTS - 1)

    def one(args):
        xblk, e = args
        return (jax.nn.silu(xblk @ w_gate[e]) * (xblk @ w_up[e])) @ w_down[e]

    yb = lax.map(one, (xb, blk_e)).reshape(p, d)
    dest = jnp.zeros((a,), jnp.int32).at[order].set(dest_sorted)
    y = yb[dest].reshape(n, k, d)
    return jnp.einsum('nk,nkd->nd', gates.astype(y.dtype), y)


def hierarchical_moe(h, w_rg, b_rg, w_re, b_re, w_gate, w_up, w_down):
    bsz, seqlen, d = h.shape
    xt = h.reshape(-1, d)
    n = xt.shape[0]
    group_prob = jax.nn.softmax((xt @ w_rg).astype(jnp.float32) + b_rg.astype(jnp.float32), axis=-1)
    p_top, g_top = lax.top_k(group_prob, 1)
    exp_logits = ((xt @ w_re).astype(jnp.float32) + b_re.astype(jnp.float32)).reshape(
        n, N_GROUPS, EXPERTS_PER_GROUP)
    in_group = exp_logits[jnp.arange(n), g_top[:, 0]]
    v_top, e_top = lax.top_k(in_group, TOPK_IN_GROUP)
    gates = p_top * jax.nn.softmax(v_top, axis=-1)
    expert = g_top * EXPERTS_PER_GROUP + e_top
    y = grouped_expert_ffn(xt, expert, gates, w_gate, w_up, w_down)
    return y.reshape(bsz, seqlen, d)


def setup_inputs(seed: int = 0) -> dict:
    key = jax.random.key(seed)
    ks = jax.random.split(key, 32)
    f32 = jnp.float32

    def nrm(k, shape, scale):
        return jax.random.normal(k, shape, f32) * scale

    L, D = DEPTH, D_MODEL
    x = nrm(ks[0], (BATCH, SEQ, D), 1.0)
    c = nrm(ks[1], (BATCH, D), 1.0)
    offset = jax.random.randint(ks[2], (BATCH, 1), 0, 4096, dtype=jnp.int32)
    positions = offset + jnp.arange(SEQ, dtype=jnp.int32)[None, :]
    return {
        'x': x,
        'c': c,
        'positions': positions,
        'w_ada': nrm(ks[3], (L, D, 6 * D), 0.5 * D ** -0.5),
        'b_ada': nrm(ks[4], (L, 6 * D), 0.01),
        'norm_mix_g': 1.0 + nrm(ks[5], (L, D), 0.02),
        'w_in': nrm(ks[6], (L, D, N_IN), D ** -0.5),
        'lambda_q1': nrm(ks[7], (L, A_HEAD_DIM), 0.1),
        'lambda_k1': nrm(ks[8], (L, A_HEAD_DIM), 0.1),
        'lambda_q2': nrm(ks[9], (L, A_HEAD_DIM), 0.1),
        'lambda_k2': nrm(ks[10], (L, A_HEAD_DIM), 0.1),
        'subln_g': 1.0 + nrm(ks[11], (L, 2 * A_HEAD_DIM), 0.02),
        'conv_w': nrm(ks[12], (L, CONV_WIDTH, CONV_CH), CONV_WIDTH ** -0.5),
        'conv_b': nrm(ks[13], (L, CONV_CH), 0.01),
        'conv_ln_g': 1.0 + nrm(ks[14], (L, CONV_CH), 0.02),
        'conv_ln_b': nrm(ks[15], (L, CONV_CH), 0.01),
        'w_out': nrm(ks[16], (L, D_MIX, D), D_MIX ** -0.5),
        'norm_ffn_g': 1.0 + nrm(ks[17], (L, D), 0.02),
        'w_router_group': nrm(ks[18], (L, D, N_GROUPS), D ** -0.5),
        'b_router_group': nrm(ks[19], (L, N_GROUPS), 0.01),
        'w_router_expert': nrm(ks[20], (L, D, N_EXPERTS), D ** -0.5),
        'b_router_expert': nrm(ks[21], (L, N_EXPERTS), 0.01),
        'w_exp_gate': nrm(ks[22], (L, N_EXPERTS, D, D_EXPERT), D ** -0.5),
        'w_exp_up': nrm(ks[23], (L, N_EXPERTS, D, D_EXPERT), D ** -0.5),
        'w_exp_down': nrm(ks[24], (L, N_EXPERTS, D_EXPERT, D), D_EXPERT ** -0.5),
        'final_norm_g': 1.0 + nrm(ks[25], (D,), 0.02),
    }


def reference(x, c, positions, w_ada, b_ada, norm_mix_g, w_in, lambda_q1, lambda_k1, lambda_q2,
              lambda_k2, subln_g, conv_w, conv_b, conv_ln_g, conv_ln_b, w_out, norm_ffn_g,
              w_router_group, b_router_group, w_router_expert, b_router_expert,
              w_exp_gate, w_exp_up, w_exp_down, final_norm_g):
    topk = min(DSA_TOPK_MAX, x.shape[1] // 4)
    rope_a = rope_tables(positions, A_HEAD_DIM)
    rope_b = rope_tables(positions, B_HEAD_DIM)
    rope_i = rope_tables(positions, IDX_DIM)
    cond = jax.nn.silu(c)
    for l in range(DEPTH):
        lam_init = 0.8 - 0.6 * math.exp(-0.3 * l)
        mod = cond @ w_ada[l] + b_ada[l]
        sh_m, sc_m, g_m, sh_f, sc_f, g_f = [m[:, None, :] for m in jnp.split(mod, 6, axis=-1)]
        h = rms_norm(x, norm_mix_g[l]) * (1.0 + sc_m) + sh_m
        x = x + g_m * hybrid_mixer(h, rope_a, rope_b, rope_i, w_in[l], w_out[l],
                                   lambda_q1[l], lambda_k1[l], lambda_q2[l], lambda_k2[l],
                                   subln_g[l], conv_w[l], conv_b[l], conv_ln_g[l], conv_ln_b[l],
                                   lam_init, topk)
        h = rms_norm(x, norm_ffn_g[l]) * (1.0 + sc_f) + sh_f
        x = x + g_f * hierarchical_moe(h, w_router_group[l], b_router_group[l],
                                       w_router_expert[l], b_router_expert[l],
                                       w_exp_gate[l], w_exp_up[l], w_exp_down[l])
    return rms_norm(x, final_norm_g)
```

```python
import functools
import math

import jax
import jax.numpy as jnp
from jax import lax
from jax.experimental import pallas as pl
from jax.experimental.pallas import tpu as pltpu

CHUNK = 64
A_HEADS = 6
A_HEAD_DIM = 64
B_HEADS = 6
B_HEAD_DIM = 128
IDX_HEADS = 8
IDX_DIM = 64
DSA_TOPK_MAX = 256
CONV_CH = 512
CONV_WIDTH = 31
N_GROUPS = 4
EXPERTS_PER_GROUP = 8
N_EXPERTS = N_GROUPS * EXPERTS_PER_GROUP
ROPE_THETA = 10000.0
NORM_EPS = 1e-6
LN_EPS = 1e-5

A_WIDTH = A_HEADS * 2 * A_HEAD_DIM
B_WIDTH = B_HEADS * B_HEAD_DIM

LANES = 128
V7X_VMEM_LIMIT_BYTES = 56 * 1024 * 1024
MASK_VALUE = -1e30

F32 = jnp.float32
BF16 = jnp.bfloat16
HIGHEST = lax.Precision.HIGHEST


def _params(semantics, vmem=V7X_VMEM_LIMIT_BYTES):
    return pltpu.CompilerParams(dimension_semantics=semantics, vmem_limit_bytes=vmem)


def _adaln_kernel(c_ref, w_ref, b_ref, o_ref):
    c = c_ref[...]
    cond = c * (1.0 / (1.0 + jnp.exp(-c)))
    w = w_ref[0]
    tn = w.shape[1]
    prod = w * jnp.tile(cond, (1, tn // LANES))
    o_ref[0] = jnp.sum(prod, axis=0, keepdims=True) + b_ref[0]


def adaln(c, w_ada, b_ada, tn=1024):
    depth, d, n = w_ada.shape
    c_b = jnp.broadcast_to(c.reshape(d, 1), (d, LANES))
    out = pl.pallas_call(
        _adaln_kernel,
        grid=(depth, n // tn),
        in_specs=[
            pl.BlockSpec((d, LANES), lambda l, j: (0, 0)),
            pl.BlockSpec((1, d, tn), lambda l, j: (l, 0, j)),
            pl.BlockSpec((1, 1, tn), lambda l, j: (l, 0, j)),
        ],
        out_specs=pl.BlockSpec((1, 1, tn), lambda l, j: (l, 0, j)),
        out_shape=jax.ShapeDtypeStruct((depth, 1, n), F32),
        compiler_params=_params(("arbitrary", "arbitrary")),
        name="adaln",
    )(c_b, w_ada, b_ada.reshape(depth, 1, n))
    return out.reshape(depth, n)


def _modulated_norm(x, g, sc, sh):
    ms = jnp.mean(x * x, axis=-1, keepdims=True)
    h = (x * lax.rsqrt(ms + NORM_EPS)) * g
    return h * (1.0 + sc) + sh


def _proj_kernel(groups, x_ref, g_ref, sc_ref, sh_ref, w_ref, ca_ref, sa_ref, cb_ref, sb_ref, *o_refs):
    hb = _modulated_norm(x_ref[...], g_ref[...], sc_ref[...], sh_ref[...]).astype(BF16)
    lane = lax.broadcasted_iota(jnp.int32, (1, LANES), 1)
    first_half = (lane % 64) < 32
    off = 0
    for (_, slabs), o_ref in zip(groups, o_refs):
        width = LANES * len(slabs)
        y = jnp.dot(hb, w_ref[:, off:off + width], preferred_element_type=F32)
        off += width
        for s, (kind, scale) in enumerate(slabs):
            ys = y[:, s * LANES:(s + 1) * LANES]
            if kind == "rope64":
                partner = jnp.where(first_half, pltpu.roll(ys, 96, 1), pltpu.roll(ys, 32, 1))
                ys = ys * ca_ref[...] + partner * sa_ref[...]
            elif kind == "rope128":
                ys = ys * cb_ref[...] + pltpu.roll(ys, 64, 1) * sb_ref[...]
            if scale != 1.0:
                ys = ys * scale
            o_ref[:, s * LANES:(s + 1) * LANES] = ys.astype(o_ref.dtype)


def proj(x, g, sc, sh, w, tables, groups, tm=512):
    t, d = x.shape
    n = w.shape[1]
    row = lambda i: (i, 0)
    fixed = lambda i: (0, 0)
    out_shapes, out_specs = [], []
    for dtype, slabs in groups:
        width = LANES * len(slabs)
        out_shapes.append(jax.ShapeDtypeStruct((t, width), dtype))
        out_specs.append(pl.BlockSpec((tm, width), row))
    vec = pl.BlockSpec((1, d), fixed)
    tab = pl.BlockSpec((tm, LANES), row)
    return pl.pallas_call(
        functools.partial(_proj_kernel, groups),
        grid=(t // tm,),
        in_specs=[pl.BlockSpec((tm, d), row), vec, vec, vec,
                  pl.BlockSpec((d, n), fixed, pipeline_mode=pl.Buffered(1)),
                  tab, tab, tab, tab],
        out_specs=out_specs,
        out_shape=out_shapes,
        compiler_params=_params(("arbitrary",)),
        name="proj_in",
    )(x, g, sc, sh, w, *tables)


def _online_update(s, m_ref, l_ref, acc_ref, vblk):
    tk = s.shape[1]
    m_prev = m_ref[...]
    m_new = jnp.maximum(m_prev, jnp.max(s, axis=1, keepdims=True))
    alpha = jnp.exp(m_prev - m_new)
    p = jnp.exp(s - jnp.tile(m_new, (1, tk // LANES)))
    l_ref[...] = alpha * l_ref[...] + jnp.sum(p, axis=1, keepdims=True)
    acc_ref[...] = alpha * acc_ref[...] + jnp.dot(p.astype(BF16), vblk, preferred_element_type=F32)
    m_ref[...] = m_new


def _init_softmax_state(m_refs, zero_refs):
    for r in m_refs:
        r[...] = jnp.full(r.shape, MASK_VALUE, F32)
    for r in zero_refs:
        r[...] = jnp.zeros(r.shape, F32)


def _attn_a_kernel(tq, out_scale, lam_ref, q_ref, kt_ref, v_ref, g_ref, o_ref, m1, l1, a1, m2, l2, a2):
    i = pl.program_id(1)
    q = q_ref[...]
    lane = lax.broadcasted_iota(jnp.int32, q.shape, 1)
    zero = jnp.zeros_like(q)
    q1 = jnp.where(lane < A_HEAD_DIM, q, zero)
    q2 = jnp.where(lane >= A_HEAD_DIM, q, zero)
    _init_softmax_state((m1, m2), (l1, l2, a1, a2))

    def step(j, visible):
        start = pl.multiple_of(j * tq, tq)
        kblk = kt_ref[:, pl.ds(start, tq)]
        vblk = v_ref[pl.ds(start, tq), :]
        for qm, m, l, a in ((q1, m1, l1, a1), (q2, m2, l2, a2)):
            s = jnp.dot(qm, kblk, preferred_element_type=F32)
            if visible is not None:
                s = jnp.where(visible, s, MASK_VALUE)
            _online_update(s, m, l, a, vblk)

    def full_block(j, carry):
        step(j, None)
        return carry

    lax.fori_loop(0, i, full_block, 0)
    r = lax.broadcasted_iota(jnp.int32, (tq, tq), 0)
    c = lax.broadcasted_iota(jnp.int32, (tq, tq), 1)
    step(i, (c // CHUNK) <= (r // CHUNK))

    lam = lam_ref[0]
    o = a1[...] / l1[...] - lam * (a2[...] / l2[...])
    ms = jnp.mean(o * o, axis=-1, keepdims=True)
    o = (o * lax.rsqrt(ms + LN_EPS)) * g_ref[...]
    o_ref[...] = (o * out_scale).astype(o_ref.dtype)


def attn_a(q, kt, v, lam, subln_g, lam_init, tq=512):
    t = q.shape[0]
    st = pltpu.VMEM((tq, LANES), F32)
    return pl.pallas_call(
        functools.partial(_attn_a_kernel, tq, 1.0 - lam_init),
        grid=(A_HEADS, t // tq),
        in_specs=[
            pl.BlockSpec(memory_space=pltpu.SMEM),
            pl.BlockSpec((tq, LANES), lambda h, i: (i, h)),
            pl.BlockSpec((LANES, t), lambda h, i: (h, 0)),
            pl.BlockSpec((t, LANES), lambda h, i: (0, h)),
            pl.BlockSpec((1, LANES), lambda h, i: (0, 0)),
        ],
        out_specs=pl.BlockSpec((tq, LANES), lambda h, i: (i, h)),
        out_shape=jax.ShapeDtypeStruct((t, A_WIDTH), BF16),
        scratch_shapes=[st] * 6,
        compiler_params=_params(("arbitrary", "arbitrary")),
        name="attn_a",
    )(lam, q, kt, v, subln_g)


_KEY_NEG_INF = -2139095041
_KEY_POS_INF = 2139095040


def _key_to_f32(k):
    bits = k ^ ((k >> 31) & jnp.int32(0x7FFFFFFF))
    return lax.bitcast_convert_type(bits, F32)


def _floor_avg(a, b):
    return (a >> 1) + (b >> 1) + (a & b & 1)


def _indexer_kernel(tq, tk, topk, iq_ref, ik_ref, iw_ref, bias_ref, s_ref, q8_ref, w8_ref, j_ref):
    i = pl.program_id(0)
    t = s_ref.shape[1]
    reps = tk // LANES
    nblk = ((i + 1) * tq + tk - 1) // tk
    kf = float(topk)

    lane = lax.broadcasted_iota(jnp.int32, (tq, LANES), 1)
    for h in range(IDX_HEADS):
        slab = iq_ref[:, (h // 2) * LANES:(h // 2 + 1) * LANES]
        keep = (lane < IDX_DIM) if h % 2 == 0 else (lane >= IDX_DIM)
        q8_ref[h * tq:(h + 1) * tq, :] = jnp.where(keep, slab, jnp.zeros_like(slab))
        w8_ref[h * tq:(h + 1) * tq, :] = jnp.broadcast_to(iw_ref[:, h:h + 1], (tq, LANES))

    row_chunk = (i * tq + lax.broadcasted_iota(jnp.int32, (tq, tk), 0)) // CHUNK
    col_local = lax.broadcasted_iota(jnp.int32, (tq, tk), 1)

    def score_block(j, carry):
        start = pl.multiple_of(j * tk, tk)
        logits = jnp.dot(q8_ref[...], ik_ref[:, pl.ds(start, tk)], preferred_element_type=F32)
        weighted = jnp.maximum(logits, 0.0) * jnp.tile(w8_ref[...], (1, reps))
        score = weighted[0:tq]
        for h in range(1, IDX_HEADS):
            score = score + weighted[h * tq:(h + 1) * tq]
        admissible = ((j * tk + col_local) // CHUNK) <= row_chunk
        s_ref[:, pl.ds(start, tk)] = jnp.where(admissible, score, -jnp.inf)
        return carry

    lax.fori_loop(0, nblk, score_block, 0)

    def count(pred):
        def body(j, cnt):
            start = pl.multiple_of(j * tk, tk)
            hit = jnp.where(pred(s_ref[:, pl.ds(start, tk)], j * tk + col_local), 1.0, 0.0)
            part = hit[:, 0:LANES]
            for r in range(1, reps):
                part = part + hit[:, r * LANES:(r + 1) * LANES]
            return cnt + part
        cnt = lax.fori_loop(0, nblk, body, jnp.zeros((tq, LANES), F32))
        return jnp.broadcast_to(jnp.sum(cnt, axis=1, keepdims=True), (tq, LANES))

    lo0 = jnp.full((tq, LANES), _KEY_NEG_INF, jnp.int32)
    hi0 = jnp.full((tq, LANES), _KEY_POS_INF, jnp.int32)
    clo0 = jnp.zeros((tq, LANES), F32) + (nblk * tk).astype(F32)
    chi0 = jnp.zeros((tq, LANES), F32)

    def active_rows(lo, hi, clo):
        return (_floor_avg(lo, hi) != lo) & (clo != kf)

    def search_cond(state):
        lo, hi, clo, _ = state
        return jnp.max(jnp.where(active_rows(lo, hi, clo), 1, 0)) > 0

    def search_body(state):
        lo, hi, clo, chi = state
        mid = _floor_avg(lo, hi)
        midf = jnp.tile(_key_to_f32(mid), (1, reps))
        c = count(lambda blk, col: blk >= midf)
        act = active_rows(lo, hi, clo)
        ge = c >= kf
        up_lo = act & ge
        up_hi = act & jnp.logical_not(ge)
        return (jnp.where(up_lo, mid, lo), jnp.where(up_hi, mid, hi),
                jnp.where(up_lo, c, clo), jnp.where(up_hi, c, chi))

    lo, hi, clo, chi = lax.while_loop(search_cond, search_body, (lo0, hi0, clo0, chi0))
    tau = _key_to_f32(lo)
    tau_b = jnp.tile(tau, (1, reps))

    tie = (clo > kf) & (tau > -jnp.inf)
    need = kf - chi
    big = jnp.int32(t)
    j_ref[...] = jnp.full((tq, LANES), big, jnp.int32)

    @pl.when(jnp.max(jnp.where(tie, 1, 0)) > 0)
    def _():
        def idx_body(_, st):
            jlo, jhi = st
            mid = _floor_avg(jlo, jhi)
            mid_b = jnp.tile(mid, (1, reps))
            c = count(lambda blk, col: (blk == tau_b) & (col <= mid_b))
            ok = c >= need
            return jnp.where(ok, jlo, mid), jnp.where(ok, mid, jhi)
        jlo0 = jnp.full((tq, LANES), -1, jnp.int32)
        jhi0 = jnp.full((tq, LANES), t - 1, jnp.int32)
        n_iter = int(math.ceil(math.log2(t))) + 1
        _, jhi = lax.fori_loop(0, n_iter, idx_body, (jlo0, jhi0))
        j_ref[...] = jnp.where(tie, jhi, big)

    jsel_b = jnp.tile(j_ref[...], (1, reps))

    def emit_block(j, carry):
        start = pl.multiple_of(j * tk, tk)
        blk = s_ref[:, pl.ds(start, tk)]
        col = j * tk + col_local
        keep = ((blk > tau_b) | ((blk == tau_b) & (col <= jsel_b))) & (blk > -jnp.inf)
        bias_ref[:, pl.ds(start, tk)] = jnp.where(keep, 0.0, MASK_VALUE).astype(bias_ref.dtype)
        return carry

    lax.fori_loop(0, nblk, emit_block, 0)

    def fill_block(j, carry):
        start = pl.multiple_of(j * tk, tk)
        bias_ref[:, pl.ds(start, tk)] = jnp.full((tq, tk), MASK_VALUE, bias_ref.dtype)
        return carry

    lax.fori_loop(nblk, t // tk, fill_block, 0)


def indexer_bias(iq, ik2t, iw, topk, tq=128, tk=512):
    t = iq.shape[0]
    return pl.pallas_call(
        functools.partial(_indexer_kernel, tq, tk, topk),
        grid=(t // tq,),
        in_specs=[
            pl.BlockSpec((tq, IDX_HEADS * IDX_DIM), lambda i: (i, 0)),
            pl.BlockSpec((LANES, t), lambda i: (0, 0)),
            pl.BlockSpec((tq, LANES), lambda i: (i, 0)),
        ],
        out_specs=pl.BlockSpec((tq, t), lambda i: (i, 0)),
        out_shape=jax.ShapeDtypeStruct((t, t), BF16),
        scratch_shapes=[
            pltpu.VMEM((tq, t), F32),
            pltpu.VMEM((IDX_HEADS * tq, LANES), BF16),
            pltpu.VMEM((IDX_HEADS * tq, LANES), F32),
            pltpu.VMEM((tq, LANES), jnp.int32),
        ],
        compiler_params=_params(("arbitrary",)),
        name="dsa_indexer",
    )(iq, ik2t, iw)


def _attn_b_kernel(tq, tk, q_ref, kt_ref, v_ref, bias_ref, o_ref, m, l, a):
    i = pl.program_id(1)
    q = q_ref[...]
    _init_softmax_state((m,), (l, a))
    nblk = ((i + 1) * tq + tk - 1) // tk

    def block(j, carry):
        start = pl.multiple_of(j * tk, tk)
        s = jnp.dot(q, kt_ref[:, pl.ds(start, tk)], preferred_element_type=F32)
        s = s + bias_ref[:, pl.ds(start, tk)].astype(F32)
        _online_update(s, m, l, a, v_ref[pl.ds(start, tk), :])
        return carry

    lax.fori_loop(0, nblk, block, 0)
    o_ref[...] = (a[...] / l[...]).astype(o_ref.dtype)


def attn_b(q, kt, v, bias, tq=256, tk=512):
    t = q.shape[0]
    st = pltpu.VMEM((tq, LANES), F32)
    return pl.pallas_call(
        functools.partial(_attn_b_kernel, tq, tk),
        grid=(B_HEADS, t // tq),
        in_specs=[
            pl.BlockSpec((tq, LANES), lambda h, i: (i, h)),
            pl.BlockSpec((LANES, t), lambda h, i: (h, 0)),
            pl.BlockSpec((t, LANES), lambda h, i: (0, h)),
            pl.BlockSpec((tq, t), lambda h, i: (i, 0)),
        ],
        out_specs=pl.BlockSpec((tq, LANES), lambda h, i: (i, h)),
        out_shape=jax.ShapeDtypeStruct((t, B_WIDTH), BF16),
        scratch_shapes=[st] * 3,
        compiler_params=_params(("arbitrary", "arbitrary")),
        name="attn_b",
    )(q, kt, v, bias)


_CONV_HALO = 32


def _glu(u):
    a = u[:, :CONV_CH]
    g = u[:, CONV_CH:]
    return a * (1.0 / (1.0 + jnp.exp(-g)))


def _conv_kernel(tt, cu_ref, halo_ref, w_ref, b_ref, lg_ref, lb_ref, o_ref, ybuf):
    i = pl.program_id(0)
    has_history = jnp.where(i > 0, 1.0, 0.0)
    ybuf[0:_CONV_HALO, :] = _glu(halo_ref[...]) * has_history
    ybuf[_CONV_HALO:, :] = _glu(cu_ref[...])
    first = _CONV_HALO - (CONV_WIDTH - 1)
    acc = jnp.zeros((tt, CONV_CH), F32) + b_ref[...]
    for j in range(CONV_WIDTH):
        acc = acc + w_ref[j:j + 1, :] * ybuf[first + j:first + j + tt, :]
    mu = jnp.mean(acc, axis=-1, keepdims=True)
    cen = acc - mu
    var = jnp.mean(cen * cen, axis=-1, keepdims=True)
    y = (cen * lax.rsqrt(var + LN_EPS)) * lg_ref[...] + lb_ref[...]
    o_ref[...] = (y * (1.0 / (1.0 + jnp.exp(-y)))).astype(o_ref.dtype)


def conv_module(cu, w, b, ln_g, ln_b, tt=512):
    t = cu.shape[0]
    fixed = lambda i: (0, 0)
    vec = pl.BlockSpec((1, CONV_CH), fixed)
    ratio = tt // _CONV_HALO
    return pl.pallas_call(
        functools.partial(_conv_kernel, tt),
        grid=(t // tt,),
        in_specs=[
            pl.BlockSpec((tt, 2 * CONV_CH), lambda i: (i, 0)),
            pl.BlockSpec((_CONV_HALO, 2 * CONV_CH), lambda i: (jnp.maximum(i * ratio - 1, 0), 0)),
            pl.BlockSpec((CONV_WIDTH, CONV_CH), fixed),
            vec, vec, vec,
        ],
        out_specs=pl.BlockSpec((tt, CONV_CH), lambda i: (i, 0)),
        out_shape=jax.ShapeDtypeStruct((t, CONV_CH), BF16),
        scratch_shapes=[pltpu.VMEM((tt + _CONV_HALO, CONV_CH), F32)],
        compiler_params=_params(("arbitrary",)),
        name="conformer_conv",
    )(cu, cu, w, b, ln_g, ln_b)


def _first_argmax(vals, lane_f):
    vmax = jnp.max(vals, axis=1, keepdims=True)
    idx = jnp.min(jnp.where(vals == vmax, lane_f, float(LANES)), axis=1, keepdims=True)
    return vmax, idx


def _outproj_kernel(ao_ref, bo_ref, co_ref, w_ref, x_ref, gm_ref, ng_ref, sc_ref, sh_ref, wr_ref, br_ref,
                    xo_ref, hf_ref, rf_ref, ri_ref, cnt_ref):
    i = pl.program_id(0)
    tm = x_ref.shape[0]

    @pl.when(i == 0)
    def _():
        cnt_ref[...] = jnp.zeros(cnt_ref.shape, F32)

    y = jnp.dot(ao_ref[...], w_ref[0:A_WIDTH, :], preferred_element_type=F32)
    y = y + jnp.dot(bo_ref[...], w_ref[A_WIDTH:A_WIDTH + B_WIDTH, :], preferred_element_type=F32)
    y = y + jnp.dot(co_ref[...], w_ref[A_WIDTH + B_WIDTH:, :], preferred_element_type=F32)
    x = x_ref[...] + gm_ref[...] * y
    xo_ref[...] = x
    hf = _modulated_norm(x, ng_ref[...], sc_ref[...], sh_ref[...])
    hf_ref[...] = hf

    logits = jnp.dot(hf, wr_ref[...], precision=HIGHEST, preferred_element_type=F32) + br_ref[...]
    lane = lax.broadcasted_iota(jnp.int32, (tm, LANES), 1)
    lane_f = lane.astype(F32)
    ninf = -jnp.inf

    gl = jnp.where(lane < N_GROUPS, logits, ninf)
    gmax, g_top = _first_argmax(gl, lane_f)
    p_top = 1.0 / jnp.sum(jnp.exp(gl - gmax), axis=1, keepdims=True)

    e_lo = N_GROUPS + EXPERTS_PER_GROUP * g_top
    in_group = (lane_f >= e_lo) & (lane_f < e_lo + EXPERTS_PER_GROUP)
    el = jnp.where(in_group, logits, ninf)
    v1, i1 = _first_argmax(el, lane_f)
    v2, i2 = _first_argmax(jnp.where(lane_f == i1, ninf, el), lane_f)
    z = jnp.exp(v2 - v1)
    gate1 = p_top / (1.0 + z)
    gate2 = p_top * z / (1.0 + z)
    e1 = i1 - float(N_GROUPS)
    e2 = i2 - float(N_GROUPS)

    hot1 = lane_f == e1
    hot2 = lane_f == e2
    onehot = jnp.where(hot1, 1.0, 0.0) + jnp.where(hot2, 1.0, 0.0)
    r = lax.broadcasted_iota(jnp.int32, (tm, tm), 0)
    c = lax.broadcasted_iota(jnp.int32, (tm, tm), 1)
    lower = jnp.where(c < r, 1.0, 0.0).astype(BF16)
    before = jnp.dot(lower, onehot.astype(BF16), preferred_element_type=F32) + cnt_ref[...]
    rank1 = jnp.sum(jnp.where(hot1, before, 0.0), axis=1, keepdims=True)
    rank2 = jnp.sum(jnp.where(hot2, before, 0.0), axis=1, keepdims=True)
    cnt_ref[...] = cnt_ref[...] + jnp.sum(onehot, axis=0, keepdims=True)

    rf_ref[...] = jnp.where(lane == 0, gate1, jnp.where(lane == 1, gate2, 0.0))
    ri = jnp.where(lane == 0, e1, jnp.where(lane == 1, e2, jnp.where(lane == 2, rank1, jnp.where(lane == 3, rank2, 0.0))))
    ri_ref[...] = ri.astype(jnp.int32)


def outproj_router(ao, bo, co, w_out, x, g_m, ng, sc, sh, w_router, b_router, tm=512):
    t, d = x.shape
    row = lambda i: (i, 0)
    fixed = lambda i: (0, 0)
    vec = pl.BlockSpec((1, d), fixed)
    lane_row = pl.BlockSpec((tm, LANES), row)
    return pl.pallas_call(
        _outproj_kernel,
        grid=(t // tm,),
        in_specs=[
            pl.BlockSpec((tm, A_WIDTH), row), pl.BlockSpec((tm, B_WIDTH), row), pl.BlockSpec((tm, CONV_CH), row),
            pl.BlockSpec((d, d), fixed, pipeline_mode=pl.Buffered(1)),
            pl.BlockSpec((tm, d), row), vec, vec, vec, vec,
            pl.BlockSpec((d, LANES), fixed), pl.BlockSpec((1, LANES), fixed),
        ],
        out_specs=[pl.BlockSpec((tm, d), row), pl.BlockSpec((tm, d), row), lane_row, lane_row,
                   pl.BlockSpec((1, LANES), fixed)],
        out_shape=[jax.ShapeDtypeStruct((t, d), F32), jax.ShapeDtypeStruct((t, d), F32),
                   jax.ShapeDtypeStruct((t, LANES), F32), jax.ShapeDtypeStruct((t, LANES), jnp.int32),
                   jax.ShapeDtypeStruct((1, LANES), F32)],
        compiler_params=_params(("arbitrary",)),
        name="outproj_router",
    )(ao, bo, co, w_out, x, g_m, ng, sc, sh, w_router, b_router)


def _row_copy(src_hbm, src_row, dst_vmem, dst_row, sem):
    return pltpu.make_async_copy(src_hbm.at[pl.ds(src_row, 1), :], dst_vmem.at[pl.ds(dst_row, 1), :], sem)


def _moe_kernel(bm, blk_e_ref, nused_ref, tok_ref, h_hbm, wg_ref, wu_ref, wd_ref, y_ref, xbuf, sem):
    del blk_e_ref
    b = pl.program_id(0)

    @pl.when(b < nused_ref[0])
    def _():
        def issue(r, carry):
            _row_copy(h_hbm, tok_ref[0, 0, r], xbuf, r, sem.at[0]).start()
            return carry

        lax.fori_loop(0, bm, issue, 0)

        def wait(r, carry):
            _row_copy(h_hbm, 0, xbuf, r, sem.at[0]).wait()
            return carry

        lax.fori_loop(0, bm, wait, 0)
        xb = xbuf[...].astype(BF16)
        gate = jnp.dot(xb, wg_ref[0], preferred_element_type=F32)
        up = jnp.dot(xb, wu_ref[0], preferred_element_type=F32)
        act = (gate * (1.0 / (1.0 + jnp.exp(-gate)))) * up
        y_ref[...] = jnp.dot(act.astype(BF16), wd_ref[0], preferred_element_type=F32)

    @pl.when(b >= nused_ref[0])
    def _():
        y_ref[...] = jnp.zeros(y_ref.shape, F32)


def moe_experts(hf, row_token, blk_e, nused, wg, wu, wd, bm):
    t, d = hf.shape
    nblk = row_token.shape[0] // bm
    de = wg.shape[2]
    grid_spec = pltpu.PrefetchScalarGridSpec(
        num_scalar_prefetch=2,
        grid=(nblk,),
        in_specs=[
            pl.BlockSpec((1, 1, bm), lambda b, be, nu: (b, 0, 0), memory_space=pltpu.SMEM),
            pl.BlockSpec(memory_space=pl.ANY),
            pl.BlockSpec((1, d, de), lambda b, be, nu: (be[b], 0, 0)),
            pl.BlockSpec((1, d, de), lambda b, be, nu: (be[b], 0, 0)),
            pl.BlockSpec((1, de, d), lambda b, be, nu: (be[b], 0, 0)),
        ],
        out_specs=pl.BlockSpec((bm, d), lambda b, be, nu: (b, 0)),
        scratch_shapes=[pltpu.VMEM((bm, d), F32), pltpu.SemaphoreType.DMA((1,))],
    )
    return pl.pallas_call(
        functools.partial(_moe_kernel, bm),
        grid_spec=grid_spec,
        out_shape=jax.ShapeDtypeStruct((nblk * bm, d), F32),
        compiler_params=_params(("arbitrary",)),
        name="moe_experts",
    )(blk_e, nused, row_token.reshape(nblk, 1, bm), hf, wg, wu, wd)


def _combine_kernel(final, d1_ref, d2_ref, x_ref, gf_ref, rf_ref, fg_ref, y_hbm, o_ref, y1, y2, sem):
    tm = x_ref.shape[0]

    def issue(r, carry):
        _row_copy(y_hbm, d1_ref[0, 0, r], y1, r, sem.at[0]).start()
        _row_copy(y_hbm, d2_ref[0, 0, r], y2, r, sem.at[1]).start()
        return carry

    lax.fori_loop(0, tm, issue, 0)

    def wait(r, carry):
        _row_copy(y_hbm, 0, y1, r, sem.at[0]).wait()
        _row_copy(y_hbm, 0, y2, r, sem.at[1]).wait()
        return carry

    lax.fori_loop(0, tm, wait, 0)
    rf = rf_ref[...]
    mix = rf[:, 0:1] * y1[...] + rf[:, 1:2] * y2[...]
    x = x_ref[...] + gf_ref[...] * mix
    if final:
        ms = jnp.mean(x * x, axis=-1, keepdims=True)
        x = (x * lax.rsqrt(ms + NORM_EPS)) * fg_ref[...]
    o_ref[...] = x


def moe_combine(x, g_f, route_f, dest1, dest2, yb, final_g, final, tm=256):
    t, d = x.shape
    row = lambda i: (i, 0)
    fixed = lambda i: (0, 0)
    idx = pl.BlockSpec((1, 1, tm), lambda i: (i, 0, 0), memory_space=pltpu.SMEM)
    vec = pl.BlockSpec((1, d), fixed)
    return pl.pallas_call(
        functools.partial(_combine_kernel, final),
        grid=(t // tm,),
        in_specs=[idx, idx, pl.BlockSpec((tm, d), row), vec, pl.BlockSpec((tm, LANES), row), vec,
                  pl.BlockSpec(memory_space=pl.ANY)],
        out_specs=pl.BlockSpec((tm, d), row),
        out_shape=jax.ShapeDtypeStruct((t, d), F32),
        scratch_shapes=[pltpu.VMEM((tm, d), F32), pltpu.VMEM((tm, d), F32), pltpu.SemaphoreType.DMA((2,))],
        compiler_params=_params(("arbitrary",)),
        name="moe_combine",
    )(dest1.reshape(t // tm, 1, tm), dest2.reshape(t // tm, 1, tm), x, g_f, route_f, final_g, yb)


def _rope_tables(positions):
    def table(dim):
        inv_freq = ROPE_THETA ** (-jnp.arange(0, dim, 2, dtype=F32) / dim)
        ang = positions.astype(F32)[:, None] * inv_freq
        reps = LANES // dim
        cos = jnp.tile(jnp.cos(ang), (1, 2 * reps))
        sin = jnp.tile(jnp.concatenate([-jnp.sin(ang), jnp.sin(ang)], axis=-1), (1, reps))
        return cos, sin
    return table(A_HEAD_DIM) + table(B_HEAD_DIM)


_GROUPS_1 = (
    (BF16, (("rope64", A_HEAD_DIM ** -0.5),) * 6),
    (BF16, (("rope64", 1.0),) * 6),
    (BF16, (("plain", 1.0),) * 6),
    (BF16, (("rope128", B_HEAD_DIM ** -0.5),) * 6),
)
_GROUPS_2 = (
    (BF16, (("rope128", 1.0),) * 6),
    (BF16, (("plain", 1.0),) * 6),
    (BF16, (("rope64", IDX_DIM ** -0.5),) * 4 + (("rope64", 1.0),)),
    (F32, (("plain", IDX_HEADS ** -0.5),)),
    (F32, (("plain", 1.0),) * 8),
)


def _pack_w_in(w):
    sizes = [A_WIDTH] * 3 + [B_WIDTH] * 3 + [IDX_HEADS * IDX_DIM, IDX_DIM, IDX_HEADS, 2 * CONV_CH]
    parts, off = [], 0
    for s in sizes:
        parts.append(w[:, off:off + s])
        off += s
    aq, ak, av, bq, bk, bv, iq, ik, iw, cu = parts
    pad = lambda a: jnp.pad(a, ((0, 0), (0, LANES - a.shape[1])))
    w1 = jnp.concatenate([aq, ak, av, bq], axis=1).astype(BF16)
    w2 = jnp.concatenate([bk, bv, iq, pad(ik), pad(iw), cu], axis=1).astype(BF16)
    return w1, w2


def _moe_plan(route_i, counts, t, bm):
    counts = counts[0, :N_EXPERTS].astype(jnp.int32)
    padded = (counts + bm - 1) // bm * bm
    pend = jnp.cumsum(padded)
    pstart = pend - padded
    p_rows = 2 * t + N_EXPERTS * bm
    nblk = p_rows // bm
    e1, e2, r1, r2 = (route_i[:, k] for k in range(4))
    dest1 = pstart[e1] + r1
    dest2 = pstart[e2] + r2
    tok = jnp.arange(t, dtype=jnp.int32)
    row_token = jnp.zeros((p_rows,), jnp.int32).at[dest1].set(tok).at[dest2].set(tok)
    blk_e = jnp.minimum(jnp.searchsorted(pend, jnp.arange(nblk, dtype=jnp.int32) * bm, side="right"),
                        N_EXPERTS - 1).astype(jnp.int32)
    nused = (pend[-1:] // bm).astype(jnp.int32)
    return dest1, dest2, row_token, blk_e, nused


def kernel(x, c, positions, w_ada, b_ada, norm_mix_g, w_in, lambda_q1, lambda_k1, lambda_q2, lambda_k2, subln_g,
           conv_w, conv_b, conv_ln_g, conv_ln_b, w_out, norm_ffn_g, w_router_group, b_router_group,
           w_router_expert, b_router_expert, w_exp_gate, w_exp_up, w_exp_down, final_norm_g):
    bsz, t, d = x.shape
    assert bsz == 1, "kernel supports a single sequence"
    depth = w_ada.shape[0]
    topk = min(DSA_TOPK_MAX, t // 4)
    moe_bm = 256

    xs = x.reshape(t, d)
    tables = _rope_tables(positions.reshape(t))
    mods = adaln(c, w_ada, b_ada).reshape(depth, 6, 1, d)
    row = lambda v: v.reshape(1, -1)

    for l in range(depth):
        lam_init = 0.8 - 0.6 * math.exp(-0.3 * l)
        sh_m, sc_m, g_m, sh_f, sc_f, g_f = (mods[l, k] for k in range(6))
        w1, w2 = _pack_w_in(w_in[l])
        ng = row(norm_mix_g[l])
        aq, ak, av, bq = proj(xs, ng, sc_m, sh_m, w1, tables, _GROUPS_1)
        bk, bv, ii, iw, cu = proj(xs, ng, sc_m, sh_m, w2, tables, _GROUPS_2)

        lam = (jnp.exp(jnp.sum(lambda_q1[l] * lambda_k1[l])) - jnp.exp(jnp.sum(lambda_q2[l] * lambda_k2[l]))
               + lam_init).reshape(1).astype(F32)
        ao = attn_a(aq, ak.T, av, lam, row(subln_g[l]), lam_init)

        ikt = ii[:, IDX_HEADS * IDX_DIM:IDX_HEADS * IDX_DIM + IDX_DIM].T
        bias = indexer_bias(ii[:, :IDX_HEADS * IDX_DIM], jnp.concatenate([ikt, ikt], axis=0), iw, topk)
        bo = attn_b(bq, bk.T, bv, bias)

        co = conv_module(cu, conv_w[l], row(conv_b[l]), row(conv_ln_g[l]), row(conv_ln_b[l]))

        w_router = jnp.pad(jnp.concatenate([w_router_group[l], w_router_expert[l]], axis=1),
                           ((0, 0), (0, LANES - N_GROUPS - N_EXPERTS)))
        b_router = jnp.pad(jnp.concatenate([b_router_group[l], b_router_expert[l]]),
                           (0, LANES - N_GROUPS - N_EXPERTS)).reshape(1, LANES)
        xs, hf, route_f, route_i, counts = outproj_router(
            ao, bo, co, w_out[l].astype(BF16), xs, g_m, row(norm_ffn_g[l]), sc_f, sh_f, w_router, b_router)

        dest1, dest2, row_token, blk_e, nused = _moe_plan(route_i, counts, t, moe_bm)
        yb = moe_experts(hf, row_token, blk_e, nused, w_exp_gate[l].astype(BF16), w_exp_up[l].astype(BF16),
                         w_exp_down[l].astype(BF16), moe_bm)
        xs = moe_combine(xs, g_f, route_f, dest1, dest2, yb, row(final_norm_g), final=(l == depth - 1))

    return xs.reshape(bsz, t, d)
```
